```python
import jax, jax.numpy as jnp
from jax import lax
import numpy as np

D_MODEL = 4096
BATCH = 4
SEQ = 4096
DEPTH = 1

MIX_WIDTH = D_MODEL
POOL_WIDTH = MIX_WIDTH // 2
CONV_WIDTH = MIX_WIDTH - POOL_WIDTH
POOL_WINDOWS = (2, 4, 8, 16)
N_POOL_GROUPS = len(POOL_WINDOWS)
POOL_GROUP_DIM = POOL_WIDTH // N_POOL_GROUPS
CONV_HEAD_DIM = 128
CONV_HEADS = CONV_WIDTH // CONV_HEAD_DIM
CONV_WIDTH_K = 3
IN_PROJ_WIDTH = POOL_WIDTH + 3 * CONV_WIDTH
D_FF = 4 * D_MODEL
N_MOD = 6
EPS = 1e-6

kernel_name = "hybrid_pool_shortconv_adaln_block"


def rmsnorm(x, g):
    xf = x.astype(jnp.float32)
    xn = xf * lax.rsqrt(jnp.mean(xf * xf, axis=-1, keepdims=True) + EPS)
    return xn.astype(x.dtype) * g


def group_rmsnorm(x, g, n_groups):
    b, s, w = x.shape
    xg = x.reshape(b, s, n_groups, w // n_groups).astype(jnp.float32)
    xn = xg * lax.rsqrt(jnp.mean(xg * xg, axis=-1, keepdims=True) + EPS)
    return xn.reshape(b, s, w).astype(x.dtype) * g


def modulate(h, shift, scale):
    return h * (1 + scale[:, None, :]) + shift[:, None, :]


def multiscale_pool(v):
    b, s, _ = v.shape
    vg = v.reshape(b, s, N_POOL_GROUPS, POOL_GROUP_DIM)
    cs = jnp.cumsum(vg.astype(jnp.float32), axis=1)
    cs = jnp.pad(cs, ((0, 0), (1, 0), (0, 0), (0, 0)))
    half = jnp.array(POOL_WINDOWS, dtype=jnp.int32) // 2
    t = jnp.arange(s, dtype=jnp.int32)[:, None]
    lo = jnp.clip(t - half[None, :], 0, s)
    hi = jnp.clip(t + half[None, :], 0, s)
    gidx = jnp.arange(N_POOL_GROUPS, dtype=jnp.int32)[None, :]
    win_sum = cs[:, hi, gidx, :] - cs[:, lo, gidx, :]
    count = (hi - lo).astype(jnp.float32)[None, :, :, None]
    out = win_sum / count - vg.astype(jnp.float32)
    return out.astype(v.dtype)


def depthwise_conv3_centred(u, w, bias):
    up = jnp.pad(u, ((0, 0), (1, 1), (0, 0)))
    return w[0] * up[:, :-2] + w[1] * up[:, 1:-1] + w[2] * up[:, 2:] + bias


def setup_inputs(seed: int = 0) -> dict:
    key = jax.random.key(seed)
    ks = jax.random.split(key, 20)
    f32 = jnp.float32
    L, D = DEPTH, D_MODEL

    def nrm(k, shape, fan_in):
        return jax.random.normal(k, shape, f32) * (fan_in ** -0.5)

    def gain(k, shape):
        return 1.0 + 0.1 * jax.random.normal(k, shape, f32)

    return {
        "x": jax.random.normal(ks[0], (BATCH, SEQ, D), f32),
        "c": jax.random.normal(ks[1], (BATCH, D), f32),
        "w_ada": nrm(ks[2], (L, D, N_MOD * D), D) * 0.5,
        "b_ada": 0.02 * jax.random.normal(ks[3], (L, N_MOD * D), f32),
        "norm1_g": gain(ks[4], (L, D)),
        "w_in": nrm(ks[5], (L, D, IN_PROJ_WIDTH), D),
        "pool_mix_w": nrm(ks[6], (L, N_POOL_GROUPS, POOL_GROUP_DIM, POOL_GROUP_DIM), POOL_GROUP_DIM),
        "pool_scale": gain(ks[7], (L, POOL_WIDTH)),
        "conv_w": nrm(ks[8], (L, CONV_WIDTH_K, CONV_WIDTH), CONV_WIDTH_K),
        "conv_b": 0.02 * jax.random.normal(ks[9], (L, CONV_WIDTH), f32),
        "gnorm_pool_g": gain(ks[10], (L, POOL_WIDTH)),
        "gnorm_conv_g": gain(ks[11], (L, CONV_WIDTH)),
        "w_out": nrm(ks[12], (L, MIX_WIDTH, D), MIX_WIDTH),
        "norm2_g": gain(ks[13], (L, D)),
        "w_mlp_in": nrm(ks[14], (L, D, D_FF), D),
        "w_mlp_out": nrm(ks[15], (L, D_FF, D), D_FF),
        "final_g": gain(ks[16], (D,)),
    }


def reference(x, c, w_ada, b_ada, norm1_g, w_in, pool_mix_w, pool_scale, conv_w, conv_b,
              gnorm_pool_g, gnorm_conv_g, w_out, norm2_g, w_mlp_in, w_mlp_out, final_g):
    c_act = jax.nn.silu(c)
    for l in range(DEPTH):
        mod = c_act @ w_ada[l] + b_ada[l]
        shift1, scale1, gate1, shift2, scale2, gate2 = jnp.split(mod, N_MOD, axis=-1)

        h = modulate(rmsnorm(x, norm1_g[l]), shift1, scale1)
        proj = jnp.einsum("bsd,de->bse", h, w_in[l])
        v_pool = proj[..., :POOL_WIDTH]
        b_gate, c_gate, u = jnp.split(proj[..., POOL_WIDTH:], 3, axis=-1)

        pooled = multiscale_pool(v_pool)
        a_out = jnp.einsum("bsgd,gde->bsge", pooled, pool_mix_w[l])
        a_out = a_out.reshape(x.shape[0], x.shape[1], POOL_WIDTH) * pool_scale[l]

        b_out = b_gate * depthwise_conv3_centred(c_gate * u, conv_w[l], conv_b[l])

        mixed = jnp.concatenate(
            [group_rmsnorm(a_out, gnorm_pool_g[l], N_POOL_GROUPS),
             group_rmsnorm(b_out, gnorm_conv_g[l], CONV_HEADS)], axis=-1)
        x = x + gate1[:, None, :] * jnp.einsum("bse,ed->bsd", mixed, w_out[l])

        h = modulate(rmsnorm(x, norm2_g[l]), shift2, scale2)
        hid = jnp.square(jax.nn.relu(jnp.einsum("bsd,df->bsf", h, w_mlp_in[l])))
        x = x + gate2[:, None, :] * jnp.einsum("bsf,fd->bsd", hid, w_mlp_out[l])

    return rmsnorm(x, final_g)
```

```python
import functools

import jax
import jax.numpy as jnp
from jax import lax
from jax.experimental import pallas as pl
from jax.experimental.pallas import tpu as pltpu

EPS = 1e-6
POOL_WINDOWS = (2, 4, 8, 16)
CONV_HEAD_DIM = 128
N_MOD = 6

V7X_VMEM_BYTES = 64 * 1024 * 1024
V7X_VMEM_HEADROOM_BYTES = 8 * 1024 * 1024
BF16_SUBLANE_TILE = 16

F32 = jnp.float32
BF16 = jnp.bfloat16


def _nbytes(shape, dtype):
    n = 1
    for s in shape:
        n *= s
    return n * jnp.dtype(dtype).itemsize


def _vmem_limit(blocks, scratch=(), temps=()):
    need = 2 * sum(_nbytes(s, d) for s, d in blocks)
    need += sum(_nbytes(s, d) for s, d in scratch)
    need += 2 * sum(_nbytes(s, d) for s, d in temps)
    need += 2 * 1024 * 1024
    return min(need, V7X_VMEM_BYTES - V7X_VMEM_HEADROOM_BYTES)


def _ada_kernel(c_ref, w_ref, b_ref, o_ref):
    c = c_ref[...]
    c_act = (c * jax.nn.sigmoid(c)).astype(BF16)
    w = w_ref[...].astype(BF16)
    o_ref[...] = jnp.dot(c_act, w, preferred_element_type=F32) + b_ref[...]


def _ada(c_pad, w_ada, b_ada, *, tn):
    m, d = c_pad.shape
    n = w_ada.shape[1]
    blocks = [((m, d), F32), ((d, tn), F32), ((1, tn), F32), ((m, tn), F32)]
    return pl.pallas_call(
        _ada_kernel,
        grid=(n // tn,),
        in_specs=[
            pl.BlockSpec((m, d), lambda j: (0, 0)),
            pl.BlockSpec((d, tn), lambda j: (0, j)),
            pl.BlockSpec((1, tn), lambda j: (0, j)),
        ],
        out_specs=pl.BlockSpec((m, tn), lambda j: (0, j)),
        out_shape=jax.ShapeDtypeStruct((m, n), F32),
        compiler_params=pltpu.CompilerParams(
            dimension_semantics=("parallel",),
            vmem_limit_bytes=_vmem_limit(blocks, temps=[((d, tn), BF16)]),
        ),
        name="ada",
    )(c_pad, w_ada, b_ada)


def _norm_mod_kernel(x_ref, g_ref, scale_ref, shift_ref, o_ref):
    x = x_ref[0]
    r = lax.rsqrt(jnp.mean(x * x, axis=-1, keepdims=True) + EPS)
    gs = g_ref[...] * (1.0 + scale_ref[0])
    o_ref[0] = (x * r * gs + shift_ref[0]).astype(o_ref.dtype)


def _norm_kernel(x_ref, g_ref, o_ref):
    x = x_ref[0]
    r = lax.rsqrt(jnp.mean(x * x, axis=-1, keepdims=True) + EPS)
    o_ref[0] = (x * r * g_ref[...]).astype(o_ref.dtype)


def _norm_mod(x, g, scale, shift, *, ts, out_dtype):
    b, s, d = x.shape
    row = pl.BlockSpec((1, ts, d), lambda bi, i: (bi, i, 0))
    vec = pl.BlockSpec((1, d), lambda bi, i: (0, 0))
    per_batch = pl.BlockSpec((1, 1, d), lambda bi, i: (bi, 0, 0))
    blocks = [((ts, d), F32), ((ts, d), out_dtype)]
    params = pltpu.CompilerParams(
        dimension_semantics=("parallel", "parallel"),
        vmem_limit_bytes=_vmem_limit(blocks, temps=[((ts, d), F32)]),
    )
    out_shape = jax.ShapeDtypeStruct((b, s, d), out_dtype)
    if scale is None:
        return pl.pallas_call(
            _norm_kernel, grid=(b, s // ts), in_specs=[row, vec], out_specs=row,
            out_shape=out_shape, compiler_params=params, name="final_norm",
        )(x, g.reshape(1, d))
    return pl.pallas_call(
        _norm_mod_kernel, grid=(b, s // ts),
        in_specs=[row, vec, per_batch, per_batch], out_specs=row,
        out_shape=out_shape, compiler_params=params, name="norm_mod",
    )(x, g.reshape(1, d), scale.reshape(b, 1, d), shift.reshape(b, 1, d))


def _mm_kernel(a_ref, w_ref, o_ref):
    acc = jnp.dot(a_ref[...], w_ref[...], preferred_element_type=F32)
    o_ref[...] = acc.astype(o_ref.dtype)


def _mm_relu2_kernel(a_ref, w_ref, o_ref):
    acc = jnp.dot(a_ref[...], w_ref[...], preferred_element_type=F32)
    r = jnp.maximum(acc, 0.0)
    o_ref[...] = (r * r).astype(o_ref.dtype)


def _mm_residual_kernel(a_ref, w_ref, x_ref, gate_ref, o_ref):
    acc = jnp.dot(a_ref[...], w_ref[...], preferred_element_type=F32)
    o_ref[...] = x_ref[...] + gate_ref[0] * acc


def _matmul(kernel, a, w, *, tm, tn, out_dtype, name):
    m, k = a.shape
    n = w.shape[1]
    blocks = [((tm, k), a.dtype), ((k, tn), w.dtype), ((tm, tn), out_dtype)]
    return pl.pallas_call(
        kernel,
        grid=(m // tm, n // tn),
        in_specs=[
            pl.BlockSpec((tm, k), lambda i, j: (i, 0)),
            pl.BlockSpec((k, tn), lambda i, j: (0, j)),
        ],
        out_specs=pl.BlockSpec((tm, tn), lambda i, j: (i, j)),
        out_shape=jax.ShapeDtypeStruct((m, n), out_dtype),
        compiler_params=pltpu.CompilerParams(
            dimension_semantics=("parallel", "parallel"),
            vmem_limit_bytes=_vmem_limit(blocks, temps=[((tm, tn), F32)]),
        ),
        name=name,
    )(a, w)


def _matmul_residual(a, w, x, gate, *, tm, tn, rows_per_batch, name):
    m, k = a.shape
    n = w.shape[1]
    tiles_per_batch = rows_per_batch // tm
    blocks = [((tm, k), a.dtype), ((k, tn), w.dtype), ((tm, tn), F32),
              ((1, tn), F32), ((tm, tn), F32)]
    return pl.pallas_call(
        _mm_residual_kernel,
        grid=(m // tm, n // tn),
        in_specs=[
            pl.BlockSpec((tm, k), lambda i, j: (i, 0)),
            pl.BlockSpec((k, tn), lambda i, j: (0, j)),
            pl.BlockSpec((tm, tn), lambda i, j: (i, j)),
            pl.BlockSpec((1, 1, tn), lambda i, j: (i // tiles_per_batch, 0, j)),
        ],
        out_specs=pl.BlockSpec((tm, tn), lambda i, j: (i, j)),
        out_shape=jax.ShapeDtypeStruct((m, n), F32),
        compiler_params=pltpu.CompilerParams(
            dimension_semantics=("parallel", "parallel"),
            vmem_limit_bytes=_vmem_limit(blocks, temps=[((tm, tn), F32)]),
        ),
        name=name,
    )(a, w, x, gate)


def _mm_acc_residual_kernel(a_ref, w_ref, x_ref, gate_ref, o_ref, acc_ref):
    kk = pl.program_id(2)

    @pl.when(kk == 0)
    def _():
        acc_ref[...] = jnp.zeros_like(acc_ref)

    acc_ref[...] += jnp.dot(a_ref[...], w_ref[...], preferred_element_type=F32)

    @pl.when(kk == pl.num_programs(2) - 1)
    def _():
        o_ref[...] = x_ref[...] + gate_ref[0] * acc_ref[...]


def _matmul_acc_residual(a, w, x, gate, *, tm, tn, tk, rows_per_batch, name):
    m, k = a.shape
    n = w.shape[1]
    tiles_per_batch = rows_per_batch // tm
    blocks = [((tm, tk), a.dtype), ((tk, tn), w.dtype), ((tm, tn), F32),
              ((1, tn), F32), ((tm, tn), F32)]
    return pl.pallas_call(
        _mm_acc_residual_kernel,
        grid=(m // tm, n // tn, k // tk),
        in_specs=[
            pl.BlockSpec((tm, tk), lambda i, j, kk: (i, kk)),
            pl.BlockSpec((tk, tn), lambda i, j, kk: (kk, j)),
            pl.BlockSpec((tm, tn), lambda i, j, kk: (i, j)),
            pl.BlockSpec((1, 1, tn), lambda i, j, kk: (i // tiles_per_batch, 0, j)),
        ],
        out_specs=pl.BlockSpec((tm, tn), lambda i, j, kk: (i, j)),
        out_shape=jax.ShapeDtypeStruct((m, n), F32),
        scratch_shapes=[pltpu.VMEM((tm, tn), F32)],
        compiler_params=pltpu.CompilerParams(
            dimension_semantics=("parallel", "parallel", "arbitrary"),
            vmem_limit_bytes=_vmem_limit(blocks, scratch=[((tm, tn), F32)],
                                         temps=[((tm, tn), F32)]),
        ),
        name=name,
    )(a, w, x, gate)


def _mixer_kernel(main_ref, prev_ref, next_ref, pmw_ref, ps_ref, gp_ref,
                  cw_ref, cb_ref, gc_ref, o_ref, cu_ref, *, seq_len, pool_width):
    i = pl.program_id(1)
    ts = main_ref.shape[1]
    halo = prev_ref.shape[1]
    first = i == 0
    last = i == pl.num_programs(1) - 1
    group = pool_width // len(POOL_WINDOWS)

    pos = i * ts + lax.broadcasted_iota(jnp.int32, (ts, 1), 0)
    out_row = lax.broadcasted_iota(jnp.int32, (ts, ts + 2 * halo), 0)
    in_row = lax.broadcasted_iota(jnp.int32, (ts, ts + 2 * halo), 1) - halo
    offset = in_row - out_row
    for g, window in enumerate(POOL_WINDOWS):
        half = window // 2
        cols = slice(g * group, (g + 1) * group)
        v = main_ref[0, :, cols]
        v_prev = prev_ref[0, :, cols]
        v_next = next_ref[0, :, cols]
        v_prev = jnp.where(first, jnp.zeros_like(v_prev), v_prev)
        v_next = jnp.where(last, jnp.zeros_like(v_next), v_next)
        ext = jnp.concatenate([v_prev, v, v_next], axis=0)
        band = ((offset >= -half) & (offset <= half - 1)).astype(BF16)
        win_sum = jnp.dot(band, ext, preferred_element_type=F32)
        lo = jnp.maximum(pos - half, 0)
        hi = jnp.minimum(pos + half, seq_len)
        count = (hi - lo).astype(F32)
        pooled = win_sum / count - v.astype(F32)
        a = jnp.dot(pooled.astype(BF16), pmw_ref[g], preferred_element_type=F32)
        a = a * ps_ref[:, cols]
        r = lax.rsqrt(jnp.mean(a * a, axis=-1, keepdims=True) + EPS)
        o_ref[0, :, cols] = (a * r * gp_ref[:, cols]).astype(o_ref.dtype)

    cw = pool_width
    b_off, c_off, u_off = pool_width, pool_width + cw, pool_width + 2 * cw

    def gated(ref):
        return (ref[0, :, c_off:c_off + cw].astype(F32)
                * ref[0, :, u_off:u_off + cw].astype(F32))

    cu_prev = gated(prev_ref)
    cu_next = gated(next_ref)
    cu_ref[0:halo, :] = jnp.where(first, jnp.zeros_like(cu_prev), cu_prev)
    cu_ref[halo:halo + ts, :] = gated(main_ref)
    cu_ref[halo + ts:halo + ts + halo, :] = jnp.where(last, jnp.zeros_like(cu_next), cu_next)

    chunk = 4 * CONV_HEAD_DIM
    for c0 in range(0, cw, chunk):
        cols = slice(c0, c0 + chunk)
        conv = (cw_ref[0:1, cols] * cu_ref[halo - 1:halo - 1 + ts, cols]
                + cw_ref[1:2, cols] * cu_ref[halo:halo + ts, cols]
                + cw_ref[2:3, cols] * cu_ref[halo + 1:halo + 1 + ts, cols]
                + cb_ref[:, cols])
        b_out = main_ref[0, :, b_off + c0:b_off + c0 + chunk].astype(F32) * conv
        for h0 in range(0, chunk, CONV_HEAD_DIM):
            head = b_out[:, h0:h0 + CONV_HEAD_DIM]
            r = lax.rsqrt(jnp.mean(head * head, axis=-1, keepdims=True) + EPS)
            gcols = slice(c0 + h0, c0 + h0 + CONV_HEAD_DIM)
            ocols = slice(pool_width + c0 + h0, pool_width + c0 + h0 + CONV_HEAD_DIM)
            o_ref[0, :, ocols] = (head * r * gc_ref[:, gcols]).astype(o_ref.dtype)


def _mixer(proj, pool_mix_w, pool_scale, gnorm_pool_g, conv_w, conv_b, gnorm_conv_g, *, ts):
    b, s, e = proj.shape
    pool_width = pool_scale.shape[-1]
    conv_width = conv_b.shape[-1]
    assert e == pool_width + 3 * conv_width and conv_width == pool_width
    halo = BF16_SUBLANE_TILE
    assert halo >= max(POOL_WINDOWS) // 2 and ts % halo == 0 and s % ts == 0
    halos_per_tile = ts // halo
    n_halo_blocks = s // halo
    mix_width = pool_width + conv_width
    group = pool_width // len(POOL_WINDOWS)

    def full(shape):
        return pl.BlockSpec(shape, lambda bi, i: (0,) * len(shape))

    blocks = [((ts, e), BF16), ((halo, e), BF16), ((halo, e), BF16),
              ((len(POOL_WINDOWS), group, group), BF16), ((ts, mix_width), BF16)]
    kernel = functools.partial(_mixer_kernel, seq_len=s, pool_width=pool_width)
    return pl.pallas_call(
        kernel,
        grid=(b, s // ts),
        in_specs=[
            pl.BlockSpec((1, ts, e), lambda bi, i: (bi, i, 0)),
            pl.BlockSpec((1, halo, e),
                         lambda bi, i: (bi, jnp.maximum(i * halos_per_tile - 1, 0), 0)),
            pl.BlockSpec((1, halo, e),
                         lambda bi, i: (bi, jnp.minimum((i + 1) * halos_per_tile,
                                                        n_halo_blocks - 1), 0)),
            full(pool_mix_w.shape),
            full((1, pool_width)), full((1, pool_width)),
            full(conv_w.shape), full((1, conv_width)), full((1, conv_width)),
        ],
        out_specs=pl.BlockSpec((1, ts, mix_width), lambda bi, i: (bi, i, 0)),
        out_shape=jax.ShapeDtypeStruct((b, s, mix_width), BF16),
        scratch_shapes=[pltpu.VMEM((ts + 2 * halo, conv_width), F32)],
        compiler_params=pltpu.CompilerParams(
            dimension_semantics=("parallel", "parallel"),
            vmem_limit_bytes=_vmem_limit(
                blocks, scratch=[((ts + 2 * halo, conv_width), F32)],
                temps=[((ts, e), F32)]),
        ),
        name="mixer",
    )(proj, proj, proj, pool_mix_w,
      pool_scale.reshape(1, pool_width), gnorm_pool_g.reshape(1, pool_width),
      conv_w, conv_b.reshape(1, conv_width), gnorm_conv_g.reshape(1, conv_width))


def kernel(x, c, w_ada, b_ada, norm1_g, w_in, pool_mix_w, pool_scale, conv_w, conv_b,
           gnorm_pool_g, gnorm_conv_g, w_out, norm2_g, w_mlp_in, w_mlp_out, final_g):
    b, s, d = x.shape
    depth = w_ada.shape[0]
    m = b * s
    for l in range(depth):
        c_pad = jnp.pad(c, ((0, 8 - b), (0, 0)))
        mod = _ada(c_pad, w_ada[l], b_ada[l].reshape(1, -1), tn=512)[:b]
        shift1, scale1, gate1, shift2, scale2, gate2 = jnp.split(mod, N_MOD, axis=-1)

        h = _norm_mod(x, norm1_g[l], scale1, shift1, ts=256, out_dtype=BF16)
        proj = _matmul(_mm_kernel, h.reshape(m, d), w_in[l].astype(BF16),
                       tm=1024, tn=1024, out_dtype=BF16, name="in_proj")
        mixed = _mixer(proj.reshape(b, s, -1), pool_mix_w[l].astype(BF16), pool_scale[l],
                       gnorm_pool_g[l], conv_w[l], conv_b[l], gnorm_conv_g[l], ts=256)
        x1 = _matmul_residual(mixed.reshape(m, -1), w_out[l].astype(BF16), x.reshape(m, d),
                              gate1.reshape(b, 1, d), tm=512, tn=1024,
                              rows_per_batch=s, name="out_proj")

        h2 = _norm_mod(x1.reshape(b, s, d), norm2_g[l], scale2, shift2, ts=256, out_dtype=BF16)
        hid = _matmul(_mm_relu2_kernel, h2.reshape(m, d), w_mlp_in[l].astype(BF16),
                      tm=1024, tn=1024, out_dtype=BF16, name="mlp_in")
        x2 = _matmul_acc_residual(hid, w_mlp_out[l].astype(BF16), x1,
                                  gate2.reshape(b, 1, d), tm=1024, tn=1024, tk=2048,
                                  rows_per_batch=s, name="mlp_out")
        x = x2.reshape(b, s, d)
    return _norm_mod(x, final_g, None, None, ts=256, out_dtype=x.dtype)
```
